```python
import math
import jax, jax.numpy as jnp
from jax import lax
import numpy as np

D_MODEL = 4096
BATCH = 8
SEQ = 2048
DEPTH = 2

CONV_CH = D_MODEL // 4
CONV_K = 31
ATTN_WIDTH = D_MODEL // 2
DIFF_HEAD_DIM = 64
ATTN_VDIM = 2 * DIFF_HEAD_DIM
ATTN_HEADS = ATTN_WIDTH // ATTN_VDIM
Q_BLOCK = 128
REL_BUCKETS = 32
REL_MAX_DIST = 128
HGRN_WIDTH = D_MODEL // 4
HGRN_EXPAND = 128
HGRN_HEADS = HGRN_WIDTH // HGRN_EXPAND
HGRN_VDIM = HGRN_WIDTH // HGRN_HEADS
HGRN_CHUNK = 64
N_BRANCH = 3
PEER_HEADS = 8
PEER_N_KEYS = 128
PEER_N_EXPERTS = PEER_N_KEYS * PEER_N_KEYS
PEER_KEY_DIM = 128
PEER_QDIM = 2 * PEER_KEY_DIM
PEER_TOPK = 16
PEER_TOKEN_BLOCK = 128
NORM_EPS = 1e-6
SPLIT_SIZES = (CONV_CH, CONV_CH,
               ATTN_WIDTH, ATTN_WIDTH, ATTN_WIDTH,
               HGRN_WIDTH, HGRN_WIDTH, HGRN_WIDTH, HGRN_WIDTH,
               N_BRANCH * D_MODEL)
IN_COLS = sum(SPLIT_SIZES)

kernel_name = 'hybrid_conv_diffattn_hgrn2_peer'


def rms_norm(x, g):
    xf = x.astype(jnp.float32)
    y = xf * lax.rsqrt(jnp.mean(xf * xf, axis=-1, keepdims=True) + NORM_EPS)
    return (y * g.astype(jnp.float32)).astype(x.dtype)


def layer_norm(x, g, b):
    xf = x.astype(jnp.float32)
    mu = jnp.mean(xf, axis=-1, keepdims=True)
    xc = xf - mu
    y = xc * lax.rsqrt(jnp.mean(xc * xc, axis=-1, keepdims=True) + NORM_EPS)
    return (y * g.astype(jnp.float32) + b.astype(jnp.float32)).astype(x.dtype)


def split_columns(p):
    points = []
    acc = 0
    for s in SPLIT_SIZES[:-1]:
        acc += s
        points.append(acc)
    return jnp.split(p, points, axis=-1)


def t5_bucket(rel):
    n = jnp.maximum(rel, 0)
    max_exact = REL_BUCKETS // 2
    nf = jnp.maximum(n, 1).astype(jnp.float32)
    large = max_exact + (jnp.log(nf / max_exact) / math.log(REL_MAX_DIST / max_exact)
                         * (REL_BUCKETS - max_exact)).astype(jnp.int32)
    large = jnp.minimum(large, REL_BUCKETS - 1)
    return jnp.where(n < max_exact, n, large)


def conformer_conv(a, b, w_dw, b_dw, ln_g, ln_b):
    u = a * jax.nn.sigmoid(b)
    u = jnp.pad(u, ((0, 0), (CONV_K - 1, 0), (0, 0)))
    y = lax.conv_general_dilated(u, w_dw[:, None, :].astype(u.dtype), window_strides=(1,), padding='VALID',
                                 dimension_numbers=('NWC', 'WIO', 'NWC'),
                                 feature_group_count=CONV_CH) + b_dw
    y = layer_norm(y, ln_g, ln_b)
    return jax.nn.silu(y)


def diff_attention(q, k, v, lam, lam_init, rel_bias, sub_g):
    B, S = q.shape[0], q.shape[1]
    q = q * (DIFF_HEAD_DIM ** -0.5)
    nblk = S // Q_BLOCK
    qb = q.reshape(B, nblk, Q_BLOCK, ATTN_HEADS, 2, DIFF_HEAD_DIM).transpose(1, 0, 2, 3, 4, 5)
    kpos = jnp.arange(S)

    def block(args):
        qi, i = args
        qpos = i * Q_BLOCK + jnp.arange(Q_BLOCK)
        rel = qpos[:, None] - kpos[None, :]
        bias = rel_bias[t5_bucket(rel)].astype(jnp.float32)
        logits = jnp.einsum('bqhmd,bkhmd->bhmqk', qi, k).astype(jnp.float32)
        logits = logits + bias.transpose(2, 0, 1)[None, :, None]
        logits = jnp.where(rel >= 0, logits, -jnp.inf)
        p = jax.nn.softmax(logits, axis=-1)
        a = p[:, :, 0] - lam * p[:, :, 1]
        return jnp.einsum('bhqk,bkhe->bqhe', a.astype(v.dtype), v)

    out = lax.map(block, (qb, jnp.arange(nblk)))
    out = out.transpose(1, 0, 2, 3, 4).reshape(B, S, ATTN_HEADS, ATTN_VDIM)
    out = rms_norm(out, sub_g) * (1.0 - lam_init)
    return out.reshape(B, S, ATTN_WIDTH)


def hgrn2(q, f_logit, inp, g, lb, out_g):
    B, S = q.shape[0], q.shape[1]
    q = jax.nn.silu(q.astype(jnp.float32))
    fl = f_logit.astype(jnp.float32)
    k = (1.0 - lb) * jax.nn.sigmoid(-fl)
    logf = jnp.logaddexp(jnp.log(lb), jnp.log1p(-lb) + jax.nn.log_sigmoid(fl))
    C = HGRN_CHUNK
    n = S // C

    def heads(t, d):
        return t.reshape(B, n, C, HGRN_HEADS, d).transpose(1, 0, 3, 2, 4)

    qc, kc, lfc = heads(q, HGRN_EXPAND), heads(k, HGRN_EXPAND), heads(logf, HGRN_EXPAND)
    ic = heads(inp.astype(jnp.float32), HGRN_VDIM)
    mask = jnp.tril(jnp.ones((C, C), dtype=bool))[:, :, None]

    def step(state, xs):
        qt, kt, it, lf = xs
        b = jnp.cumsum(lf, axis=2)
        diff = b[:, :, :, None, :] - b[:, :, None, :, :]
        decay = jnp.where(mask, jnp.exp(jnp.where(mask, diff, 0.0)), 0.0)
        A = jnp.einsum('bhtd,bhsd,bhtsd->bhts', qt, kt, decay)
        o = jnp.einsum('bhts,bhsv->bhtv', A, it) + jnp.einsum('bhtd,bhdv->bhtv', qt * jnp.exp(b), state)
        bC = b[:, :, -1:, :]
        new_state = jnp.exp(bC[:, :, 0, :, None]) * state + jnp.einsum('bhsd,bhsv->bhdv', kt * jnp.exp(bC - b), it)
        return new_state, o

    s0 = jnp.zeros((B, HGRN_HEADS, HGRN_EXPAND, HGRN_VDIM), jnp.float32)
    _, o = lax.scan(step, s0, (qc, kc, ic, lfc))
    o = o.transpose(1, 0, 3, 2, 4).reshape(B, S, HGRN_HEADS, HGRN_VDIM)
    o = rms_norm(o, out_g).reshape(B, S, HGRN_WIDTH)
    return (o * jax.nn.silu(g.astype(jnp.float32))).astype(g.dtype)


def peer(h, w_q, sub_keys, U, V):
    B, S, D = h.shape
    T = B * S
    TB = PEER_TOKEN_BLOCK
    hb_all = h.reshape(T // TB, TB, D)

    def block(hb):
        q = (hb @ w_q).reshape(TB, PEER_HEADS, 2, PEER_KEY_DIM)
        s = jnp.einsum('thpk,hpnk->thpn', q, sub_keys).astype(jnp.float32)
        top_s, top_i = lax.top_k(s, PEER_TOPK)
        cand_s = (top_s[:, :, 0, :, None] + top_s[:, :, 1, None, :]).reshape(TB, PEER_HEADS, PEER_TOPK * PEER_TOPK)
        cand_i = (top_i[:, :, 0, :, None] * PEER_N_KEYS + top_i[:, :, 1, None, :]).reshape(TB, PEER_HEADS, PEER_TOPK * PEER_TOPK)
        best_s, pos = lax.top_k(cand_s, PEER_TOPK)
        idx = jnp.take_along_axis(cand_i, pos, axis=-1)
        gate = jax.nn.softmax(best_s, axis=-1)
        u = jnp.take(U, idx, axis=0)
        act = jax.nn.gelu(jnp.einsum('td,thkd->thk', hb, u).astype(jnp.float32), approximate=False)
        coef = (gate * act).astype(hb.dtype)
        v = jnp.take(V, idx, axis=0)
        return jnp.einsum('thk,thkd->td', coef, v)

    return lax.map(block, hb_all).reshape(B, S, D)


def setup_inputs(seed: int = 0) -> dict:
    key = jax.random.key(seed)
    ks = jax.random.split(key, 32)

    def nrm(k, shape, scale):
        return jax.random.normal(k, shape, jnp.float32) * scale

    L = DEPTH
    return {
        'x': nrm(ks[0], (BATCH, SEQ, D_MODEL), 1.0),
        'norm1_g': 1.0 + nrm(ks[1], (L, D_MODEL), 0.02),
        'w_in': nrm(ks[2], (L, D_MODEL, IN_COLS), D_MODEL ** -0.5),
        'conv_w': nrm(ks[3], (L, CONV_K, CONV_CH), CONV_K ** -0.5),
        'conv_b': nrm(ks[4], (L, CONV_CH), 0.02),
        'conv_ln_g': 1.0 + nrm(ks[5], (L, CONV_CH), 0.02),
        'conv_ln_b': nrm(ks[6], (L, CONV_CH), 0.02),
        'w_conv_out': nrm(ks[7], (L, CONV_CH, D_MODEL), CONV_CH ** -0.5),
        'lam_q1': nrm(ks[8], (L, DIFF_HEAD_DIM), 0.1),
        'lam_k1': nrm(ks[9], (L, DIFF_HEAD_DIM), 0.1),
        'lam_q2': nrm(ks[10], (L, DIFF_HEAD_DIM), 0.1),
        'lam_k2': nrm(ks[11], (L, DIFF_HEAD_DIM), 0.1),
        'attn_sub_g': 1.0 + nrm(ks[12], (L, ATTN_VDIM), 0.02),
        'w_attn_out': nrm(ks[13], (L, ATTN_WIDTH, D_MODEL), ATTN_WIDTH ** -0.5),
        'hgrn_lb_logits': nrm(ks[14], (L, HGRN_WIDTH), 0.5),
        'hgrn_out_g': 1.0 + nrm(ks[15], (L, HGRN_VDIM), 0.02),
        'w_hgrn_out': nrm(ks[16], (L, HGRN_WIDTH, D_MODEL), HGRN_WIDTH ** -0.5),
        'rel_bias': nrm(ks[17], (REL_BUCKETS, ATTN_HEADS), 0.5),
        'w_o': nrm(ks[18], (L, D_MODEL, D_MODEL), D_MODEL ** -0.5),
        'norm2_g': 1.0 + nrm(ks[19], (L, D_MODEL), 0.02),
        'peer_w_q': nrm(ks[20], (L, D_MODEL, PEER_HEADS * PEER_QDIM), D_MODEL ** -0.5),
        'peer_sub_keys': nrm(ks[21], (L, PEER_HEADS, 2, PEER_N_KEYS, PEER_KEY_DIM), PEER_KEY_DIM ** -0.5),
        'peer_u': nrm(ks[22], (L, PEER_N_EXPERTS, D_MODEL), D_MODEL ** -0.5),
        'peer_v': nrm(ks[23], (L, PEER_N_EXPERTS, D_MODEL), PEER_HEADS ** -0.5),
        'final_g': 1.0 + nrm(ks[24], (D_MODEL,), 0.02),
    }


def reference(x, norm1_g, w_in, conv_w, conv_b, conv_ln_g, conv_ln_b, w_conv_out,
              lam_q1, lam_k1, lam_q2, lam_k2, attn_sub_g, w_attn_out,
              hgrn_lb_logits, hgrn_out_g, w_hgrn_out, rel_bias, w_o, norm2_g,
              peer_w_q, peer_sub_keys, peer_u, peer_v, final_g):
    B, S, _ = x.shape
    sm = jax.nn.softmax(hgrn_lb_logits.astype(jnp.float32), axis=0)
    lb_all = jnp.clip(jnp.cumsum(sm, axis=0) - sm[0:1], 0.0, 1.0 - 1e-6)
    for l in range(DEPTH):
        lam_init = 0.8 - 0.6 * math.exp(-0.3 * l)
        h = rms_norm(x, norm1_g[l])
        proj = h @ w_in[l]
        c_a, c_b, a_q, a_k, a_v, r_q, r_f, r_i, r_g, gates = split_columns(proj)
        y_conv = conformer_conv(c_a, c_b, conv_w[l], conv_b[l], conv_ln_g[l], conv_ln_b[l]) @ w_conv_out[l]
        lam = (jnp.exp(jnp.sum(lam_q1[l] * lam_k1[l]).astype(jnp.float32))
               - jnp.exp(jnp.sum(lam_q2[l] * lam_k2[l]).astype(jnp.float32)) + lam_init)
        y_attn = diff_attention(a_q.reshape(B, S, ATTN_HEADS, 2, DIFF_HEAD_DIM),
                                a_k.reshape(B, S, ATTN_HEADS, 2, DIFF_HEAD_DIM),
                                a_v.reshape(B, S, ATTN_HEADS, ATTN_VDIM),
                                lam, lam_init, rel_bias, attn_sub_g[l]) @ w_attn_out[l]
        y_hgrn = hgrn2(r_q, r_f, r_i, r_g, lb_all[l], hgrn_out_g[l]) @ w_hgrn_out[l]
        g = jax.nn.sigmoid(gates.reshape(B, S, N_BRANCH, D_MODEL))
        merged = g[:, :, 0] * y_conv + g[:, :, 1] * y_attn + g[:, :, 2] * y_hgrn
        x = x + merged @ w_o[l]
        h2 = rms_norm(x, norm2_g[l])
        x = x + peer(h2, peer_w_q[l], peer_sub_keys[l], peer_u[l], peer_v[l])
    return rms_norm(x, final_g)
```

```python
import functools
import math

import jax
import jax.numpy as jnp
from jax import lax
from jax.experimental import pallas as pl
from jax.experimental.pallas import tpu as pltpu

NORM_EPS = 1e-6
CONV_K = 31
DIFF_HEAD_DIM = 64
ATTN_VDIM = 2 * DIFF_HEAD_DIM
REL_BUCKETS = 32
REL_MAX_DIST = 128
HGRN_EXPAND = 128
HGRN_CHUNK = 64
HGRN_SUB = 16
PEER_HEADS = 8
PEER_N_KEYS = 128
PEER_KEY_DIM = 128
PEER_TOPK = 16

LANES = 128
V7X_VMEM_LIMIT = 56 * 1024 * 1024

F32 = jnp.float32
BF16 = jnp.bfloat16
NEG_INF = float("-inf")


def _params(*sem):
    return pltpu.CompilerParams(dimension_semantics=sem, vmem_limit_bytes=V7X_VMEM_LIMIT)


def _rms_kernel(x_ref, g_ref, o_ref):
    x = x_ref[...]
    ms = jnp.mean(x * x, axis=-1, keepdims=True)
    o_ref[...] = (x * lax.rsqrt(ms + NORM_EPS) * g_ref[...]).astype(o_ref.dtype)


def rms_rows(x, g, out_dtype, tm=256):
    T, D = x.shape
    tm = min(tm, T)
    return pl.pallas_call(
        _rms_kernel,
        grid=(T // tm,),
        in_specs=[pl.BlockSpec((tm, D), lambda i: (i, 0)),
                  pl.BlockSpec((1, D), lambda i: (0, 0))],
        out_specs=pl.BlockSpec((tm, D), lambda i: (i, 0)),
        out_shape=jax.ShapeDtypeStruct((T, D), out_dtype),
        compiler_params=_params("parallel"),
        name="rms_rows",
    )(x, g.reshape(1, D).astype(F32))


def _mm_kernel(a_ref, w_ref, o_ref):
    o_ref[...] = jnp.dot(a_ref[...], w_ref[...], preferred_element_type=F32).astype(o_ref.dtype)


def _mm_res_kernel(a_ref, w_ref, r_ref, o_ref):
    acc = jnp.dot(a_ref[...], w_ref[...], preferred_element_type=F32)
    o_ref[...] = (r_ref[...] + acc).astype(o_ref.dtype)


def _mm_acc_kernel(a_ref, w_ref, o_ref, acc_ref, *, nk):
    k = pl.program_id(2)

    @pl.when(k == 0)
    def _():
        acc_ref[...] = jnp.zeros_like(acc_ref)

    acc_ref[...] += jnp.dot(a_ref[...], w_ref[...], preferred_element_type=F32)

    @pl.when(k == nk - 1)
    def _():
        o_ref[...] = acc_ref[...].astype(o_ref.dtype)


def matmul(a, w, out_dtype, res=None, tm=1024, tn=512, tk=None, name="matmul"):
    M, K = a.shape
    _, N = w.shape
    tm, tn = min(tm, M), min(tn, N)
    if tk is None or tk >= K:
        in_specs = [pl.BlockSpec((tm, K), lambda i, j: (i, 0)),
                    pl.BlockSpec((K, tn), lambda i, j: (0, j))]
        args = [a, w]
        body = _mm_kernel
        if res is not None:
            in_specs.append(pl.BlockSpec((tm, tn), lambda i, j: (i, j)))
            args.append(res)
            body = _mm_res_kernel
        return pl.pallas_call(
            body,
            grid=(M // tm, N // tn),
            in_specs=in_specs,
            out_specs=pl.BlockSpec((tm, tn), lambda i, j: (i, j)),
            out_shape=jax.ShapeDtypeStruct((M, N), out_dtype),
            compiler_params=_params("parallel", "parallel"),
            name=name,
        )(*args)
    assert res is None
    nk = K // tk
    return pl.pallas_call(
        functools.partial(_mm_acc_kernel, nk=nk),
        grid=(M // tm, N // tn, nk),
        in_specs=[pl.BlockSpec((tm, tk), lambda i, j, k: (i, k)),
                  pl.BlockSpec((tk, tn), lambda i, j, k: (k, j))],
        out_specs=pl.BlockSpec((tm, tn), lambda i, j, k: (i, j)),
        out_shape=jax.ShapeDtypeStruct((M, N), out_dtype),
        scratch_shapes=[pltpu.VMEM((tm, tn), F32)],
        compiler_params=_params("parallel", "parallel", "arbitrary"),
        name=name,
    )(a, w)


CONV_HALO = 32


def _conv_kernel(a_ref, b_ref, ap_ref, bp_ref, w_ref, cb_ref, g_ref, beta_ref, o_ref, u_ref, y_ref, *, ts, C):
    s = pl.program_id(1)
    u_prev = ap_ref[0] * jax.nn.sigmoid(bp_ref[0])
    u_ref[0:CONV_HALO, :] = jnp.where(s > 0, u_prev, 0.0)
    u_ref[CONV_HALO:CONV_HALO + ts, :] = a_ref[0] * jax.nn.sigmoid(b_ref[0])
    base = CONV_HALO - (CONV_K - 1)
    for c in range(C // LANES):
        cs = slice(c * LANES, (c + 1) * LANES)
        acc = jnp.broadcast_to(cb_ref[:, cs], (ts, LANES))
        for j in range(CONV_K):
            acc = acc + w_ref[j:j + 1, cs] * u_ref[base + j:base + j + ts, cs]
        y_ref[:, cs] = acc
    y = y_ref[...]
    mu = jnp.mean(y, axis=-1, keepdims=True)
    yc = y - mu
    yn = yc * lax.rsqrt(jnp.mean(yc * yc, axis=-1, keepdims=True) + NORM_EPS)
    yn = yn * g_ref[...] + beta_ref[...]
    o_ref[0] = (yn * jax.nn.sigmoid(yn)).astype(o_ref.dtype)


def conv_branch(proj_c, w_dw, b_dw, ln_g, ln_b, ts=256):
    B, S, C2 = proj_c.shape
    C = C2 // 2
    ts = min(ts, S)
    hb = ts // CONV_HALO
    w_pad = jnp.zeros((CONV_HALO, C), F32).at[:CONV_K].set(w_dw.astype(F32))
    row = lambda v: v.reshape(1, C).astype(F32)
    cur = lambda col: pl.BlockSpec((1, ts, C), lambda b, s: (b, s, col))
    prev = lambda col: pl.BlockSpec((1, CONV_HALO, C), lambda b, s: (b, jnp.maximum(s * hb - 1, 0), col))
    vec = pl.BlockSpec((1, C), lambda b, s: (0, 0))
    return pl.pallas_call(
        functools.partial(_conv_kernel, ts=ts, C=C),
        grid=(B, S // ts),
        in_specs=[cur(0), cur(1), prev(0), prev(1),
                  pl.BlockSpec((CONV_HALO, C), lambda b, s: (0, 0)), vec, vec, vec],
        out_specs=pl.BlockSpec((1, ts, C), lambda b, s: (b, s, 0)),
        out_shape=jax.ShapeDtypeStruct((B, S, C), BF16),
        scratch_shapes=[pltpu.VMEM((CONV_HALO + ts, C), F32), pltpu.VMEM((ts, C), F32)],
        compiler_params=_params("parallel", "parallel"),
        name="conv_branch",
    )(proj_c, proj_c, proj_c, proj_c, w_pad, row(b_dw), row(ln_g), row(ln_b))


def _t5_bucket(rel):
    n = jnp.maximum(rel, 0)
    max_exact = REL_BUCKETS // 2
    nf = jnp.maximum(n, 1).astype(F32)
    large = max_exact + (jnp.log(nf / max_exact) / math.log(REL_MAX_DIST / max_exact)
                         * (REL_BUCKETS - max_exact)).astype(jnp.int32)
    large = jnp.minimum(large, REL_BUCKETS - 1)
    return jnp.where(n < max_exact, n, large)


def _bias_tiles(rel_bias, t):
    assert t >= REL_MAX_DIST
    d = jnp.arange(t)[:, None] - jnp.arange(t)[None, :]
    tab = rel_bias.astype(F32).T
    diag = jnp.where(d >= 0, tab[:, _t5_bucket(d)], NEG_INF)
    off1 = tab[:, _t5_bucket(d + t)]
    far = jnp.broadcast_to(tab[:, REL_BUCKETS - 1][:, None, None], off1.shape)
    return jnp.stack([diag, off1, far], axis=1)


def _attn_kernel(q_ref, k_ref, v_ref, b_ref, lam_ref, g_ref, o_ref, kt_ref, *, t, nblk, scale, out_scale):
    qi = pl.program_id(2)

    @pl.when(qi == 0)
    def _():
        for j in range(nblk):
            kt_ref[j] = k_ref[0, j * t:(j + 1) * t, :].astype(F32).T.astype(BF16)

    q = q_ref[0] * scale
    lane = lax.broadcasted_iota(jnp.int32, q.shape, 1)
    q_lo = jnp.where(lane < DIFF_HEAD_DIM, q, jnp.zeros_like(q))
    q_hi = jnp.where(lane >= DIFF_HEAD_DIM, q, jnp.zeros_like(q))

    def update(qm, kt, vb, bias, m, l, acc):
        s = jnp.dot(qm, kt, preferred_element_type=F32) + bias
        m_new = jnp.maximum(m, jnp.max(s, axis=-1, keepdims=True))
        alpha = jnp.exp(m - m_new)
        p = jnp.exp(s - m_new)
        l_new = alpha * l + jnp.sum(p, axis=-1, keepdims=True)
        acc_new = alpha * acc + jnp.dot(p.astype(BF16), vb, preferred_element_type=F32)
        return m_new, l_new, acc_new

    def body(j, carry):
        m1, l1, a1, m2, l2, a2 = carry
        kt = kt_ref[j]
        vb = v_ref[0, pl.ds(pl.multiple_of(j * t, t), t), :]
        bias = b_ref[0, jnp.minimum(qi - j, 2)]
        m1, l1, a1 = update(q_lo, kt, vb, bias, m1, l1, a1)
        m2, l2, a2 = update(q_hi, kt, vb, bias, m2, l2, a2)
        return m1, l1, a1, m2, l2, a2

    m0 = jnp.full((t, 1), NEG_INF, F32)
    l0 = jnp.zeros((t, 1), F32)
    a0 = jnp.zeros((t, ATTN_VDIM), F32)
    _, l1, a1, _, l2, a2 = lax.fori_loop(0, qi + 1, body, (m0, l0, a0, m0, l0, a0))
    o = a1 / l1 - lam_ref[...] * (a2 / l2)
    o = o * lax.rsqrt(jnp.mean(o * o, axis=-1, keepdims=True) + NORM_EPS) * g_ref[...]
    o_ref[0] = (o * out_scale).astype(o_ref.dtype)


def diff_attention(qkv, bias_tiles, lam, sub_g, lam_init, t=256):
    B, S, W3 = qkv.shape
    W = W3 // 3
    H = W // ATTN_VDIM
    t = min(t, S)
    nblk = S // t
    lam_row = jnp.full((1, ATTN_VDIM), lam, F32)
    return pl.pallas_call(
        functools.partial(_attn_kernel, t=t, nblk=nblk, scale=DIFF_HEAD_DIM ** -0.5, out_scale=1.0 - lam_init),
        grid=(B, H, nblk),
        in_specs=[pl.BlockSpec((1, t, ATTN_VDIM), lambda b, h, i: (b, i, h)),
                  pl.BlockSpec((1, S, ATTN_VDIM), lambda b, h, i: (b, 0, H + h)),
                  pl.BlockSpec((1, S, ATTN_VDIM), lambda b, h, i: (b, 0, 2 * H + h)),
                  pl.BlockSpec((1, 3, t, t), lambda b, h, i: (h, 0, 0, 0)),
                  pl.BlockSpec((1, ATTN_VDIM), lambda b, h, i: (0, 0)),
                  pl.BlockSpec((1, ATTN_VDIM), lambda b, h, i: (0, 0))],
        out_specs=pl.BlockSpec((1, t, ATTN_VDIM), lambda b, h, i: (b, i, h)),
        out_shape=jax.ShapeDtypeStruct((B, S, W), BF16),
        scratch_shapes=[pltpu.VMEM((nblk, ATTN_VDIM, t), BF16)],
        compiler_params=_params("parallel", "parallel", "arbitrary"),
        name="diff_attention",
    )(qkv, qkv, qkv, bias_tiles, lam_row, sub_g.reshape(1, ATTN_VDIM).astype(F32))


def _hgrn_kernel(q_ref, f_ref, i_ref, g_ref, lb_ref, og_ref, o_ref, at_ref, *, n_chunks):
    C, SUB = HGRN_CHUNK, HGRN_SUB
    lb = lb_ref[...]
    log_lb = jnp.log(lb)
    log1m_lb = jnp.log1p(-lb)
    one_m_lb = 1.0 - lb
    row = lax.broadcasted_iota(jnp.int32, (C, C), 0)
    col = lax.broadcasted_iota(jnp.int32, (C, C), 1)
    tril = (row >= col).astype(F32)
    srow = lax.broadcasted_iota(jnp.int32, (C, LANES), 0)
    tlane = {ns: lax.broadcasted_iota(jnp.int32, (ns, C), 1) for ns in range(SUB, C + 1, SUB)}

    def chunk(c, state_t):
        r0 = pl.multiple_of(c * C, C)
        rq = q_ref[0, pl.ds(r0, C), :]
        fl = f_ref[0, pl.ds(r0, C), :]
        iv = i_ref[0, pl.ds(r0, C), :]
        gv = g_ref[0, pl.ds(r0, C), :]
        q = rq * jax.nn.sigmoid(rq)
        k = one_m_lb * jax.nn.sigmoid(-fl)
        log_sig = jnp.minimum(fl, 0.0) - jnp.log1p(jnp.exp(-jnp.abs(fl)))
        x2 = log1m_lb + log_sig
        logf = jnp.maximum(log_lb, x2) + jnp.log1p(jnp.exp(-jnp.abs(log_lb - x2)))
        b = jnp.dot(tril, logf, preferred_element_type=F32, precision=lax.Precision.HIGHEST)

        at_ref[...] = jnp.zeros_like(at_ref)
        for t in range(C):
            ns = SUB * (t // SUB + 1)
            mask = srow[:ns] <= t
            diff = b[t:t + 1, :] - b[:ns]
            e = jnp.where(mask, jnp.exp(jnp.where(mask, diff, 0.0)), 0.0)
            colv = jnp.sum(q[t:t + 1, :] * k[:ns] * e, axis=1, keepdims=True)
            at_ref[0:ns, :] = jnp.where(tlane[ns] == t, colv, at_ref[0:ns, :])
        a_t = at_ref[...]
        o = lax.dot_general(a_t.astype(BF16), iv.astype(BF16), (((0,), (0,)), ((), ())),
                            preferred_element_type=F32)
        o = o + lax.dot_general((q * jnp.exp(b)).astype(BF16), state_t.astype(BF16),
                                (((1,), (1,)), ((), ())), preferred_element_type=F32)
        b_last = b[C - 1:C, :]
        kd = k * jnp.exp(b_last - b)
        state_t = state_t * jnp.exp(b_last) + lax.dot_general(
            iv.astype(BF16), kd.astype(BF16), (((0,), (0,)), ((), ())), preferred_element_type=F32)
        on = o * lax.rsqrt(jnp.mean(o * o, axis=-1, keepdims=True) + NORM_EPS) * og_ref[...]
        o_ref[0, pl.ds(r0, C), :] = (on * (gv * jax.nn.sigmoid(gv))).astype(o_ref.dtype)
        return state_t

    lax.fori_loop(0, n_chunks, chunk, jnp.zeros((LANES, HGRN_EXPAND), F32))


def hgrn_branch(proj_h, lb, out_g):
    B, S, W4 = proj_h.shape
    W = W4 // 4
    H = W // HGRN_EXPAND
    blk = lambda part: pl.BlockSpec((1, S, HGRN_EXPAND), lambda b, h: (b, 0, part * H + h))
    return pl.pallas_call(
        functools.partial(_hgrn_kernel, n_chunks=S // HGRN_CHUNK),
        grid=(B, H),
        in_specs=[blk(0), blk(1), blk(2), blk(3),
                  pl.BlockSpec((1, HGRN_EXPAND), lambda b, h: (0, h)),
                  pl.BlockSpec((1, HGRN_EXPAND), lambda b, h: (0, 0))],
        out_specs=pl.BlockSpec((1, S, HGRN_EXPAND), lambda b, h: (b, 0, h)),
        out_shape=jax.ShapeDtypeStruct((B, S, W), BF16),
        scratch_shapes=[pltpu.VMEM((HGRN_CHUNK, HGRN_CHUNK), F32)],
        compiler_params=_params("parallel", "parallel"),
        name="hgrn_branch",
    )(proj_h, proj_h, proj_h, proj_h, lb.reshape(1, W).astype(F32), out_g.reshape(1, HGRN_EXPAND).astype(F32))


def _merge_kernel(yc_ref, ya_ref, yh_ref, wc_ref, wa_ref, wh_ref, gc_ref, ga_ref, gh_ref, o_ref):
    def branch(y_ref, w_ref, gate_ref):
        return jax.nn.sigmoid(gate_ref[...].astype(F32)) * jnp.dot(y_ref[...], w_ref[...], preferred_element_type=F32)

    o_ref[...] = (branch(yc_ref, wc_ref, gc_ref) + branch(ya_ref, wa_ref, ga_ref)
                  + branch(yh_ref, wh_ref, gh_ref)).astype(o_ref.dtype)


def merge_branches(yc, ya, yh, wc, wa, wh, gates, tm=1024, tn=512):
    T = yc.shape[0]
    D = wc.shape[1]
    tm, tn = min(tm, T), min(tn, D)
    nj = D // tn
    act = lambda y: pl.BlockSpec((tm, y.shape[1]), lambda i, j: (i, 0))
    wgt = lambda w: pl.BlockSpec((w.shape[0], tn), lambda i, j: (0, j))
    gate = lambda br: pl.BlockSpec((tm, tn), lambda i, j: (i, br * nj + j))
    return pl.pallas_call(
        _merge_kernel,
        grid=(T // tm, nj),
        in_specs=[act(yc), act(ya), act(yh), wgt(wc), wgt(wa), wgt(wh), gate(0), gate(1), gate(2)],
        out_specs=pl.BlockSpec((tm, tn), lambda i, j: (i, j)),
        out_shape=jax.ShapeDtypeStruct((T, D), BF16),
        compiler_params=_params("parallel", "parallel"),
        name="merge_branches",
    )(yc, ya, yh, wc, wa, wh, gates, gates, gates)


_PEER_PAIRS = [(a, b) for a in range(PEER_TOPK) for b in range(PEER_TOPK) if (a + 1) * (b + 1) <= PEER_TOPK]
_PEER_CAND_ROWS = -(-len(_PEER_PAIRS) // 8) * 8


def _extract_topk(x, iota, k, on_pick):
    for a in range(k):
        mx = jnp.max(x, axis=0, keepdims=True)
        idx = jnp.min(jnp.where(x == mx, iota, float(x.shape[0])), axis=0, keepdims=True)
        sel = iota == idx
        on_pick(a, mx, sel)
        x = jnp.where(sel, NEG_INF, x)


def _peer_select_kernel(qt_ref, keys_ref, r2_ref, nb_ref, e1_ref, e2_ref, cand_ref, selm_ref, *, tt):
    K = PEER_TOPK
    iota_n = lax.broadcasted_iota(jnp.int32, (PEER_N_KEYS, tt), 0).astype(F32)
    iota_c = lax.broadcasted_iota(jnp.int32, (_PEER_CAND_ROWS, tt), 0).astype(F32)

    def ranked(s):
        vals = [None] * K
        rank = [jnp.full(s.shape, float(K), F32)]

        def pick(a, mx, sel):
            vals[a] = mx
            rank[0] = jnp.where(sel, float(a), rank[0])

        _extract_topk(s, iota_n, K, pick)
        return vals, rank[0]

    def head(h, carry):
        r0 = pl.multiple_of(h * 2 * PEER_KEY_DIM, 2 * PEER_KEY_DIM)
        q1 = qt_ref[pl.ds(r0, PEER_KEY_DIM), :]
        q2 = qt_ref[pl.ds(r0 + PEER_KEY_DIM, PEER_KEY_DIM), :]
        s1 = jnp.dot(keys_ref[2 * h], q1, preferred_element_type=F32, precision=lax.Precision.HIGHEST)
        s2 = jnp.dot(keys_ref[2 * h + 1], q2, preferred_element_type=F32, precision=lax.Precision.HIGHEST)
        v1, rank1 = ranked(s1)
        v2, rank2 = ranked(s2)

        cand_ref[...] = jnp.full(cand_ref.shape, NEG_INF, F32)
        for r, (a, b) in enumerate(_PEER_PAIRS):
            cand_ref[r:r + 1, :] = v1[a] + v2[b]
        cand = cand_ref[...]
        selm = [jnp.zeros(cand.shape, F32)]

        def pick2(a, mx, sel):
            selm[0] = jnp.where(sel, 1.0, selm[0])

        _extract_topk(cand, iota_c, K, pick2)
        selm_ref[...] = selm[0]
        top = v1[0] + v2[0]
        z = jnp.sum(jnp.where(selm[0] > 0.0, jnp.exp(cand - top), 0.0), axis=0, keepdims=True)

        nb = jnp.zeros(s1.shape, F32)
        r = 0
        for a in range(K):
            cnt = sum(1 for (pa, _) in _PEER_PAIRS if pa == a)
            nb_a = jnp.sum(selm_ref[r:r + cnt, :], axis=0, keepdims=True)
            nb = jnp.where(rank1 == float(a), nb_a, nb)
            r += cnt
        r2_ref[h] = rank2
        nb_ref[h] = nb
        e1_ref[h] = jnp.exp(s1 - v1[0]) / z
        e2_ref[h] = jnp.exp(s2 - v2[0])
        return carry

    lax.fori_loop(0, PEER_HEADS, head, 0)


def peer_select(qt, sub_keys, tt=256):
    R, T = qt.shape
    tt = min(tt, T)
    keys = sub_keys.reshape(PEER_HEADS * 2, PEER_N_KEYS, PEER_KEY_DIM).astype(F32)
    out = jax.ShapeDtypeStruct((PEER_HEADS, PEER_N_KEYS, T), F32)
    ospec = pl.BlockSpec((PEER_HEADS, PEER_N_KEYS, tt), lambda i: (0, 0, i))
    return pl.pallas_call(
        functools.partial(_peer_select_kernel, tt=tt),
        grid=(T // tt,),
        in_specs=[pl.BlockSpec((R, tt), lambda i: (0, i)),
                  pl.BlockSpec(keys.shape, lambda i: (0, 0, 0))],
        out_specs=[ospec] * 4,
        out_shape=[out] * 4,
        scratch_shapes=[pltpu.VMEM((_PEER_CAND_ROWS, tt), F32), pltpu.VMEM((_PEER_CAND_ROWS, tt), F32)],
        compiler_params=_params("parallel"),
        name="peer_select",
    )(qt, keys)


PEER_ACT_GROUPS = 8


def _peer_act_kernel(u_ref, ht_ref, r2_ref, nb_ref, e1_ref, e2_ref, o_ref, *, tt):
    act = jnp.dot(u_ref[...], ht_ref[...], preferred_element_type=F32)
    gel = 0.5 * act * (1.0 + lax.erf(act * (2.0 ** -0.5)))
    for lc in range(tt // LANES):
        cols = slice(lc * LANES, (lc + 1) * LANES)
        for c in range(PEER_ACT_GROUPS):
            rows = slice(c * PEER_N_KEYS, (c + 1) * PEER_N_KEYS)
            w = jnp.zeros((PEER_N_KEYS, LANES), F32)
            for h in range(PEER_HEADS):
                nb = nb_ref[h, c:c + 1, cols]
                e1 = e1_ref[h, c:c + 1, cols]
                w = w + jnp.where(r2_ref[h, :, cols] < nb, e2_ref[h, :, cols] * e1, 0.0)
            o_ref[rows, cols] = (w * gel[rows, cols]).astype(o_ref.dtype)


def peer_coefficients(u, ht, tables, tt=512):
    N, D = u.shape
    T = ht.shape[1]
    tt = min(tt, T)
    te = PEER_ACT_GROUPS * PEER_N_KEYS
    tspec = pl.BlockSpec((PEER_HEADS, PEER_N_KEYS, tt), lambda i, n: (0, 0, i))
    kspec = pl.BlockSpec((PEER_HEADS, PEER_ACT_GROUPS, tt), lambda i, n: (0, n, i))
    return pl.pallas_call(
        functools.partial(_peer_act_kernel, tt=tt),
        grid=(T // tt, N // te),
        in_specs=[pl.BlockSpec((te, D), lambda i, n: (n, 0)),
                  pl.BlockSpec((D, tt), lambda i, n: (0, i)),
                  tspec, kspec, kspec, tspec],
        out_specs=pl.BlockSpec((te, tt), lambda i, n: (n, i)),
        out_shape=jax.ShapeDtypeStruct((N, T), BF16),
        compiler_params=_params("parallel", "arbitrary"),
        name="peer_coefficients",
    )(u, ht, *tables)


def kernel(x, norm1_g, w_in, conv_w, conv_b, conv_ln_g, conv_ln_b, w_conv_out, lam_q1, lam_k1, lam_q2, lam_k2, attn_sub_g, w_attn_out, hgrn_lb_logits, hgrn_out_g, w_hgrn_out, rel_bias, w_o, norm2_g, peer_w_q, peer_sub_keys, peer_u, peer_v, final_g):
    B, S, D = x.shape
    T = B * S
    depth = w_in.shape[0]
    conv_ch = conv_w.shape[2]
    attn_w = w_attn_out.shape[1]
    hgrn_w = w_hgrn_out.shape[1]
    c_end = 2 * conv_ch
    a_end = c_end + 3 * attn_w
    h_end = a_end + 4 * hgrn_w

    sm = jax.nn.softmax(hgrn_lb_logits.astype(F32), axis=0)
    lb_all = jnp.clip(jnp.cumsum(sm, axis=0) - sm[0:1], 0.0, 1.0 - 1e-6)
    attn_t = min(256, S)
    bias_tiles = _bias_tiles(rel_bias, attn_t)

    x = x.reshape(T, D)
    for l in range(depth):
        lam_init = 0.8 - 0.6 * math.exp(-0.3 * l)
        lam = (jnp.exp(jnp.sum(lam_q1[l] * lam_k1[l]).astype(F32))
               - jnp.exp(jnp.sum(lam_q2[l] * lam_k2[l]).astype(F32)) + lam_init)
        w = w_in[l]
        h = rms_rows(x, norm1_g[l], BF16)
        proj_c = matmul(h, w[:, :c_end].astype(BF16), F32, name="in_proj_conv")
        qkv = matmul(h, w[:, c_end:a_end].astype(BF16), BF16, name="in_proj_qkv")
        proj_h = matmul(h, w[:, a_end:h_end].astype(BF16), F32, name="in_proj_hgrn")
        gates = matmul(h, w[:, h_end:].astype(BF16), BF16, name="in_proj_gates")

        y_conv = conv_branch(proj_c.reshape(B, S, c_end), conv_w[l], conv_b[l], conv_ln_g[l], conv_ln_b[l])
        y_attn = diff_attention(qkv.reshape(B, S, 3 * attn_w), bias_tiles, lam, attn_sub_g[l], lam_init, t=attn_t)
        y_hgrn = hgrn_branch(proj_h.reshape(B, S, 4 * hgrn_w), lb_all[l], hgrn_out_g[l])
        merged = merge_branches(y_conv.reshape(T, conv_ch), y_attn.reshape(T, attn_w), y_hgrn.reshape(T, hgrn_w),
                                w_conv_out[l].astype(BF16), w_attn_out[l].astype(BF16), w_hgrn_out[l].astype(BF16),
                                gates)
        x = matmul(merged, w_o[l].astype(BF16), F32, res=x, name="out_proj")

        h2t = rms_rows(x, norm2_g[l], BF16).T
        qt = matmul(peer_w_q[l].T.astype(BF16), h2t, F32, name="peer_query")
        tables = peer_select(qt, peer_sub_keys[l])
        coef_t = peer_coefficients(peer_u[l].astype(BF16), h2t, tables)
        out_t = matmul(peer_v[l].T.astype(BF16), coef_t, F32, tm=1024, tn=1024, tk=1024, name="peer_out")
        x = x + out_t.T
    return rms_rows(x, final_g, F32).reshape(B, S, D)
```
